```python
import math
import jax, jax.numpy as jnp
from jax import lax
import numpy as np

D_MODEL = 1024
BATCH = 2
SEQ = 8192
DEPTH = 1

CHUNK = 64
Q_BLOCK = 128
EPS = 1e-6
DA_HEADS = 4
DA_QK_DIM = 64
DA_V_DIM = 2 * DA_QK_DIM
DA_WIDTH = DA_HEADS * DA_V_DIM
HG_HEADS = 4
HG_K = 128
HG_V = 128
HG_WIDTH = HG_HEADS * HG_V
D_FF = -(-8 * D_MODEL // (3 * 256)) * 256

IN_COLS = [
    DA_HEADS * 2 * DA_QK_DIM,
    DA_HEADS * 2 * DA_QK_DIM,
    DA_WIDTH,
    HG_HEADS * HG_K,
    HG_HEADS * HG_K,
    HG_WIDTH,
    HG_WIDTH,
    D_MODEL,
    D_MODEL,
]
IN_WIDTH = int(sum(IN_COLS))
IN_SPLITS = [int(s) for s in np.cumsum(IN_COLS)[:-1]]

kernel_name = "hybrid_diffattn_hgrn2_gated_block"


def rmsnorm(x, g):
    xf = x.astype(jnp.float32)
    y = xf * lax.rsqrt(jnp.mean(xf * xf, axis=-1, keepdims=True) + EPS)
    return (y * g.astype(jnp.float32)).astype(x.dtype)


def diff_attention(q, k, v, lam):
    B, H, _, S, _ = q.shape
    scale = DA_QK_DIM ** -0.5
    kchunk = jnp.arange(S) // CHUNK

    def block(i):
        qb = lax.dynamic_slice_in_dim(q, i * Q_BLOCK, Q_BLOCK, axis=3)
        s = jnp.einsum('bhmqd,bhmkd->bhmqk', qb, k).astype(jnp.float32) * scale
        qchunk = (i * Q_BLOCK + jnp.arange(Q_BLOCK)) // CHUNK
        mask = kchunk[None, :] <= qchunk[:, None]
        s = jnp.where(mask, s, jnp.finfo(jnp.float32).min)
        p = jax.nn.softmax(s, axis=-1)
        a = p[:, :, 0] - lam * p[:, :, 1]
        return jnp.einsum('bhqk,bhkd->bhqd', a.astype(v.dtype), v)

    out = lax.map(block, jnp.arange(S // Q_BLOCK))
    return out.transpose(1, 0, 3, 2, 4).reshape(B, S, H, v.shape[-1])


def to_chunks(t, d):
    B, S, _ = t.shape
    return t.reshape(B, S // CHUNK, CHUNK, -1, d).transpose(0, 3, 1, 2, 4)


def hgrn2(q_raw, f_raw, i_in, lb):
    B, S, _ = q_raw.shape
    q = jax.nn.silu(q_raw.astype(jnp.float32))
    log_f = jnp.log(lb + (1.0 - lb) * jax.nn.sigmoid(f_raw.astype(jnp.float32)))
    k = 1.0 - jnp.exp(log_f)
    q, log_f, k = to_chunks(q, HG_K), to_chunks(log_f, HG_K), to_chunks(k, HG_K)
    v = to_chunks(i_in.astype(jnp.float32), HG_V)

    b = jnp.cumsum(log_f, axis=3)
    b_last = b[:, :, :, -1:]
    qd = q * jnp.exp(b)
    kd = k * jnp.exp(-b)
    causal = jnp.tril(jnp.ones((CHUNK, CHUNK), dtype=bool))
    A = jnp.where(causal, jnp.einsum('bhnck,bhnsk->bhncs', qd, kd), 0.0)
    o_intra = jnp.einsum('bhncs,bhnsv->bhncv', A, v)

    k_end = k * jnp.exp(b_last - b)
    D = jnp.einsum('bhnck,bhncv->bhnkv', k_end, v)
    g = jnp.exp(b_last[:, :, :, 0])

    def step(state, inp):
        g_n, D_n = inp
        return g_n[..., None] * state + D_n, state

    init = jnp.zeros((B, HG_HEADS, HG_K, HG_V), jnp.float32)
    _, S_prev = lax.scan(step, init, (g.transpose(2, 0, 1, 3), D.transpose(2, 0, 1, 3, 4)))
    S_prev = S_prev.transpose(1, 2, 0, 3, 4)
    o = o_intra + jnp.einsum('bhnck,bhnkv->bhncv', qd, S_prev)
    return o.transpose(0, 2, 3, 1, 4).reshape(B, S, HG_HEADS, HG_V)


def setup_inputs(seed: int = 0) -> dict:
    key = jax.random.key(seed)
    ks = jax.random.split(key, 16)
    f32 = jnp.float32

    def w(k, shape, fan_in):
        return jax.random.normal(k, shape, f32) * fan_in ** -0.5

    def gain(k, shape):
        return 1.0 + 0.01 * jax.random.normal(k, shape, f32)

    return {
        "x": jax.random.normal(ks[0], (BATCH, SEQ, D_MODEL), f32),
        "norm_mix": gain(ks[1], (DEPTH, D_MODEL)),
        "w_in": w(ks[2], (DEPTH, D_MODEL, IN_WIDTH), D_MODEL),
        "da_lambda": 0.1 * jax.random.normal(ks[3], (DEPTH, 4, DA_QK_DIM), f32),
        "da_subln": gain(ks[4], (DEPTH, DA_V_DIM)),
        "hg_lower_bound": 0.1 * jax.random.normal(ks[5], (DEPTH + 1, HG_HEADS * HG_K), f32),
        "hg_norm": gain(ks[6], (DEPTH, HG_V)),
        "w_da_out": w(ks[7], (DEPTH, DA_WIDTH, D_MODEL), DA_WIDTH),
        "w_hg_out": w(ks[8], (DEPTH, HG_WIDTH, D_MODEL), HG_WIDTH),
        "w_out": w(ks[9], (DEPTH, D_MODEL, D_MODEL), D_MODEL),
        "norm_ffn": gain(ks[10], (DEPTH, D_MODEL)),
        "w_ffn_in": w(ks[11], (DEPTH, D_MODEL, 2 * D_FF), D_MODEL),
        "w_ffn_out": w(ks[12], (DEPTH, D_FF, D_MODEL), D_FF),
        "norm_final": gain(ks[13], (D_MODEL,)),
    }


def reference(x, norm_mix, w_in, da_lambda, da_subln, hg_lower_bound, hg_norm,
              w_da_out, w_hg_out, w_out, norm_ffn, w_ffn_in, w_ffn_out, norm_final):
    B, S, _ = x.shape
    lb_all = jnp.cumsum(jax.nn.softmax(hg_lower_bound.astype(jnp.float32), axis=0), axis=0)
    h = x
    for l in range(DEPTH):
        u = rmsnorm(h, norm_mix[l])
        proj = u @ w_in[l]
        q_da, k_da, v_da, q_hg, f_hg, i_hg, og_hg, g_da, g_hg = jnp.split(proj, IN_SPLITS, axis=-1)

        lam_init = 0.8 - 0.6 * math.exp(-0.3 * l)
        lp = da_lambda[l].astype(jnp.float32)
        lam = jnp.exp(jnp.sum(lp[0] * lp[1])) - jnp.exp(jnp.sum(lp[2] * lp[3])) + lam_init
        qh = q_da.reshape(B, S, DA_HEADS, 2, DA_QK_DIM).transpose(0, 2, 3, 1, 4)
        kh = k_da.reshape(B, S, DA_HEADS, 2, DA_QK_DIM).transpose(0, 2, 3, 1, 4)
        vh = v_da.reshape(B, S, DA_HEADS, DA_V_DIM).transpose(0, 2, 1, 3)
        o_da = diff_attention(qh, kh, vh, lam)
        o_da = (rmsnorm(o_da, da_subln[l]) * (1.0 - lam_init)).reshape(B, S, DA_WIDTH)
        y_da = o_da.astype(x.dtype) @ w_da_out[l]

        o_hg = hgrn2(q_hg, f_hg, i_hg, lb_all[l])
        og = og_hg.reshape(B, S, HG_HEADS, HG_V).astype(jnp.float32)
        o_hg = (rmsnorm(o_hg, hg_norm[l]) * jax.nn.silu(og)).reshape(B, S, HG_WIDTH)
        y_hg = o_hg.astype(x.dtype) @ w_hg_out[l]

        merged = jax.nn.sigmoid(g_da) * y_da + jax.nn.sigmoid(g_hg) * y_hg
        h = h + merged @ w_out[l]

        z = rmsnorm(h, norm_ffn[l])
        gate, up = jnp.split(z @ w_ffn_in[l], 2, axis=-1)
        h = h + (jax.nn.silu(gate) * up) @ w_ffn_out[l]
    return rmsnorm(h, norm_final)
```

```python
import functools
import math

import jax
import jax.numpy as jnp
from jax import lax
from jax.experimental import pallas as pl
from jax.experimental.pallas import tpu as pltpu

F32 = jnp.float32
BF16 = jnp.bfloat16

D_MODEL = 1024
CHUNK = 64
EPS = 1e-6
DA_HEADS = 4
DA_QK_DIM = 64
DA_V_DIM = 128
DA_WIDTH = DA_HEADS * DA_V_DIM
HG_HEADS = 4
HG_K = 128
HG_V = 128
HG_WIDTH = HG_HEADS * HG_V
D_FF = 2816
LAM_INIT = 0.8 - 0.6 * math.exp(-0.3 * 0)

C_Q, C_K, C_V, C_HQ, C_HF, C_HI, C_OG, C_GDA, C_GHG, C_END = (
    0, 512, 1024, 1536, 2048, 2560, 3072, 3584, 4608, 5632)

VMEM_LIMIT = 56 * 1024 * 1024
NEG = float(jnp.finfo(jnp.float32).min)

_NT = (((1,), (1,)), ((), ()))
_TN = (((0,), (0,)), ((), ()))


def _const_spec(shape):
    nd = len(shape)
    return pl.BlockSpec(shape, lambda *_: (0,) * nd, pipeline_mode=pl.Buffered(1))


def _inproj_kernel(x_ref, g_ref, w_ref, wvt_ref,
                   q_ref, k_ref, vt_ref, hq_ref, hf_ref, hi_ref, og_ref, gates_ref):
    x = x_ref[...]
    ms = jnp.mean(x * x, axis=-1, keepdims=True)
    u = (x * lax.rsqrt(ms + EPS) * g_ref[...]).astype(BF16)

    def seg(c0, c1):
        return jnp.dot(u, w_ref[:, c0:c1], preferred_element_type=F32)

    q_ref[...] = (seg(C_Q, C_K) * (DA_QK_DIM ** -0.5)).astype(BF16)
    k_ref[...] = seg(C_K, C_V).astype(BF16)
    vt_ref[0] = lax.dot_general(wvt_ref[...], u, _NT,
                                preferred_element_type=F32).astype(BF16)
    hq_ref[...] = seg(C_HQ, C_HF).astype(BF16)
    hf_ref[...] = seg(C_HF, C_HI)
    hi_ref[...] = seg(C_HI, C_OG).astype(BF16)
    og_ref[...] = seg(C_OG, C_GDA).astype(BF16)
    gates_ref[:, 0:1024] = seg(C_GDA, C_GHG).astype(BF16)
    gates_ref[:, 1024:2048] = seg(C_GHG, C_END).astype(BF16)


def _inproj(x2, g, w, wvt, B, S, tm):
    T = B * S
    ns = S // tm
    row = lambda b, s: (b * ns + s, 0)
    tok = lambda n, dt: jax.ShapeDtypeStruct((T, n), dt)
    return pl.pallas_call(
        _inproj_kernel,
        grid=(B, ns),
        in_specs=[
            pl.BlockSpec((tm, D_MODEL), row),
            _const_spec((1, D_MODEL)),
            _const_spec((D_MODEL, C_END)),
            _const_spec((DA_WIDTH, D_MODEL)),
        ],
        out_specs=[
            pl.BlockSpec((tm, 512), row),
            pl.BlockSpec((tm, 512), row),
            pl.BlockSpec((1, DA_WIDTH, tm), lambda b, s: (b, 0, s)),
            pl.BlockSpec((tm, 512), row),
            pl.BlockSpec((tm, 512), row),
            pl.BlockSpec((tm, 512), row),
            pl.BlockSpec((tm, 512), row),
            pl.BlockSpec((tm, 2048), row),
        ],
        out_shape=[
            tok(512, BF16), tok(512, BF16),
            jax.ShapeDtypeStruct((B, DA_WIDTH, S), BF16),
            tok(512, BF16), tok(512, F32), tok(512, BF16), tok(512, BF16),
            tok(2048, BF16),
        ],
        compiler_params=pltpu.CompilerParams(
            dimension_semantics=("arbitrary", "arbitrary"),
            vmem_limit_bytes=VMEM_LIMIT),
        name="inproj",
    )(x2, g, w, wvt)


def _attn_kernel(q_ref, k_ref, vt_ref, lam_ref, subln_ref, o_ref,
                 m_scr, l_scr, acc_scr, *, tq):
    qi = pl.program_id(2)
    q = q_ref[...]
    lane = lax.broadcasted_iota(jnp.int32, q.shape, 1)
    zero = jnp.zeros_like(q)
    qs = (jnp.where(lane < DA_QK_DIM, q, zero), jnp.where(lane >= DA_QK_DIM, q, zero))

    m_scr[...] = jnp.full(m_scr.shape, NEG, F32)
    l_scr[...] = jnp.zeros(l_scr.shape, F32)
    acc_scr[...] = jnp.zeros(acc_scr.shape, F32)

    def tile(kt, masked):
        start = pl.multiple_of(kt * tq, tq)
        k = k_ref[pl.ds(start, tq), :]
        vt = vt_ref[0, :, pl.ds(start, tq)]
        if masked:
            krow = lax.broadcasted_iota(jnp.int32, (tq, tq), 0) // CHUNK
            qcol = lax.broadcasted_iota(jnp.int32, (tq, tq), 1) // CHUNK
            vis = krow <= qcol
        for i in range(2):
            s = lax.dot_general(k, qs[i], _NT, preferred_element_type=F32)
            if masked:
                s = jnp.where(vis, s, NEG)
            m_old = m_scr[i]
            m_new = jnp.maximum(m_old, jnp.max(s, axis=0, keepdims=True))
            alpha = jnp.exp(m_old - m_new)
            p = jnp.exp(s - m_new)
            l_scr[i] = alpha * l_scr[i] + jnp.sum(p, axis=0, keepdims=True)
            pv = jnp.dot(vt, p.astype(BF16), preferred_element_type=F32)
            acc_scr[i] = alpha * acc_scr[i] + pv
            m_scr[i] = m_new

    def body(kt, c):
        tile(kt, False)
        return c

    lax.fori_loop(0, qi, body, 0)
    tile(qi, True)

    lp = lam_ref[...]
    lam = (jnp.exp(jnp.sum(lp[0:1] * lp[1:2], axis=1, keepdims=True))
           - jnp.exp(jnp.sum(lp[2:3] * lp[3:4], axis=1, keepdims=True)) + LAM_INIT)
    o = acc_scr[0] * (1.0 / l_scr[0]) - lam * (acc_scr[1] * (1.0 / l_scr[1]))
    ms = jnp.mean(o * o, axis=0, keepdims=True)
    y = o * lax.rsqrt(ms + EPS) * subln_ref[...] * (1.0 - LAM_INIT)
    o_ref[...] = y.T.astype(BF16)


def _attention(q, k, vt, lam, subln_col, B, S, tq):
    T = B * S
    nq = S // tq
    kern = functools.partial(_attn_kernel, tq=tq)
    return pl.pallas_call(
        kern,
        grid=(B, DA_HEADS, nq),
        in_specs=[
            pl.BlockSpec((tq, 128), lambda b, h, i: (b * nq + i, h)),
            pl.BlockSpec((S, 128), lambda b, h, i: (b, h)),
            pl.BlockSpec((1, 128, S), lambda b, h, i: (b, h, 0)),
            _const_spec((4, DA_QK_DIM)),
            _const_spec((DA_V_DIM, 1)),
        ],
        out_specs=pl.BlockSpec((tq, 128), lambda b, h, i: (b * nq + i, h)),
        out_shape=jax.ShapeDtypeStruct((T, DA_WIDTH), BF16),
        scratch_shapes=[
            pltpu.VMEM((2, 1, tq), F32),
            pltpu.VMEM((2, 1, tq), F32),
            pltpu.VMEM((2, DA_V_DIM, tq), F32),
        ],
        compiler_params=pltpu.CompilerParams(
            dimension_semantics=("arbitrary", "arbitrary", "arbitrary"),
            vmem_limit_bytes=VMEM_LIMIT),
        name="diffattn",
    )(q, k, vt, lam, subln_col)


def _split3(a):
    hi = a.astype(BF16)
    r = a - hi.astype(F32)
    mid = r.astype(BF16)
    lo = (r - mid.astype(F32)).astype(BF16)
    return hi, mid, lo


def _hgrn_kernel(hq_ref, hf_ref, hi_ref, og_ref, lbraw_ref, gn_ref, o_ref, st_scr, *, tc):
    @pl.when(pl.program_id(1) == 0)
    def _():
        st_scr[...] = jnp.zeros(st_scr.shape, F32)

    nchunk = tc // CHUNK
    lbr = lbraw_ref[...]
    e = jnp.exp(lbr - jnp.max(lbr, axis=0, keepdims=True))
    lb = e[0:1] / jnp.sum(e, axis=0, keepdims=True)

    hq = hq_ref[...].astype(F32)
    q = hq * jax.nn.sigmoid(hq)
    f = lb + (1.0 - lb) * jax.nn.sigmoid(hf_ref[...])
    logf = jnp.log(f)
    kk = 1.0 - f
    v = hi_ref[...]

    r = lax.broadcasted_iota(jnp.int32, (tc, tc), 0)
    c = lax.broadcasted_iota(jnp.int32, (tc, tc), 1)
    tril = (r >= c) & ((r // CHUNK) == (c // CHUNK))
    ltri = tril.astype(BF16)
    b = sum(jnp.dot(ltri, part, preferred_element_type=F32) for part in _split3(logf))

    qd = (q * jnp.exp(b)).astype(BF16)
    kd = (kk * jnp.exp(-b)).astype(BF16)

    og = og_ref[...].astype(F32)
    gate = og * jax.nn.sigmoid(og)
    gn = gn_ref[...]

    for h in range(HG_HEADS):
        hs = slice(h * HG_K, (h + 1) * HG_K)
        a = lax.dot_general(qd[:, hs], kd[:, hs], _NT, preferred_element_type=F32)
        a = jnp.where(tril, a, 0.0).astype(BF16)
        o_intra = jnp.dot(a, v[:, hs], preferred_element_type=F32)
        for n in range(nchunk):
            rs = slice(n * CHUNK, (n + 1) * CHUNK)
            b_c = b[rs, hs]
            b_last = b_c[CHUNK - 1:CHUNK, :]
            st = st_scr[h]
            o_inter = lax.dot_general(qd[rs, hs], st.astype(BF16), _NT,
                                      preferred_element_type=F32)
            o = o_intra[rs, :] + o_inter
            k_end = (kk[rs, hs] * jnp.exp(b_last - b_c)).astype(BF16)
            d_t = lax.dot_general(v[rs, hs], k_end, _TN, preferred_element_type=F32)
            st_scr[h] = st * jnp.exp(b_last) + d_t
            ms = jnp.mean(o * o, axis=-1, keepdims=True)
            y = o * lax.rsqrt(ms + EPS) * gn * gate[rs, hs]
            o_ref[rs, hs] = y.astype(BF16)


def _hgrn(hq, hf, hi, og, lbraw, gn, B, S, tc):
    T = B * S
    ns = S // tc
    row = lambda b, s: (b * ns + s, 0)
    kern = functools.partial(_hgrn_kernel, tc=tc)
    return pl.pallas_call(
        kern,
        grid=(B, ns),
        in_specs=[
            pl.BlockSpec((tc, 512), row),
            pl.BlockSpec((tc, 512), row),
            pl.BlockSpec((tc, 512), row),
            pl.BlockSpec((tc, 512), row),
            _const_spec((2, HG_WIDTH)),
            _const_spec((1, HG_V)),
        ],
        out_specs=pl.BlockSpec((tc, 512), row),
        out_shape=jax.ShapeDtypeStruct((T, HG_WIDTH), BF16),
        scratch_shapes=[pltpu.VMEM((HG_HEADS, HG_V, HG_K), F32)],
        compiler_params=pltpu.CompilerParams(
            dimension_semantics=("arbitrary", "arbitrary"),
            vmem_limit_bytes=VMEM_LIMIT),
        name="hgrn2",
    )(hq, hf, hi, og, lbraw, gn)


def _rms(x, g):
    ms = jnp.mean(x * x, axis=-1, keepdims=True)
    return x * lax.rsqrt(ms + EPS) * g


def _post_kernel(x_ref, oda_ref, ohg_ref, gates_ref, wda_ref, whg_ref, wout_ref,
                 nffn_ref, wfi_ref, wfo_ref, nfin_ref, out_ref, act_scr, *, fc):
    y_da = jnp.dot(oda_ref[...], wda_ref[...], preferred_element_type=F32)
    y_hg = jnp.dot(ohg_ref[...], whg_ref[...], preferred_element_type=F32)
    g_da = jax.nn.sigmoid(gates_ref[:, 0:1024].astype(F32))
    g_hg = jax.nn.sigmoid(gates_ref[:, 1024:2048].astype(F32))
    merged = (g_da * y_da + g_hg * y_hg).astype(BF16)
    h = x_ref[...] + jnp.dot(merged, wout_ref[...], preferred_element_type=F32)
    z = _rms(h, nffn_ref[...]).astype(BF16)
    for c0 in range(0, D_FF, fc):
        gate = jnp.dot(z, wfi_ref[:, c0:c0 + fc], preferred_element_type=F32)
        up = jnp.dot(z, wfi_ref[:, D_FF + c0:D_FF + c0 + fc], preferred_element_type=F32)
        act_scr[:, c0:c0 + fc] = (gate * jax.nn.sigmoid(gate) * up).astype(BF16)
    h = h + jnp.dot(act_scr[...], wfo_ref[...], preferred_element_type=F32)
    out_ref[...] = _rms(h, nfin_ref[...])


def _post(x2, oda, ohg, gates, wda, whg, wout, nffn, wfi, wfo, nfin, tm, fc):
    T = x2.shape[0]
    row = lambda i: (i, 0)
    kern = functools.partial(_post_kernel, fc=fc)
    return pl.pallas_call(
        kern,
        grid=(T // tm,),
        in_specs=[
            pl.BlockSpec((tm, D_MODEL), row),
            pl.BlockSpec((tm, DA_WIDTH), row),
            pl.BlockSpec((tm, HG_WIDTH), row),
            pl.BlockSpec((tm, 2048), row),
            _const_spec((DA_WIDTH, D_MODEL)),
            _const_spec((HG_WIDTH, D_MODEL)),
            _const_spec((D_MODEL, D_MODEL)),
            _const_spec((1, D_MODEL)),
            _const_spec((D_MODEL, 2 * D_FF)),
            _const_spec((D_FF, D_MODEL)),
            _const_spec((1, D_MODEL)),
        ],
        out_specs=pl.BlockSpec((tm, D_MODEL), row),
        out_shape=jax.ShapeDtypeStruct((T, D_MODEL), F32),
        scratch_shapes=[pltpu.VMEM((tm, D_FF), BF16)],
        compiler_params=pltpu.CompilerParams(
            dimension_semantics=("arbitrary",),
            vmem_limit_bytes=VMEM_LIMIT),
        name="post",
    )(x2, oda, ohg, gates, wda, whg, wout, nffn, wfi, wfo, nfin)


def kernel(x, norm_mix, w_in, da_lambda, da_subln, hg_lower_bound, hg_norm,
           w_da_out, w_hg_out, w_out, norm_ffn, w_ffn_in, w_ffn_out, norm_final):
    B, S, D = x.shape
    T = B * S
    x2 = x.reshape(T, D)
    w_in_b = w_in[0].astype(BF16)
    wvt = w_in[0][:, C_V:C_HQ].T.astype(BF16)

    q, k, vt, hq, hf, hi, og, gates = _inproj(
        x2, norm_mix[0].reshape(1, D), w_in_b, wvt, B, S, tm=512)

    oda = _attention(q, k, vt, da_lambda[0], da_subln[0].reshape(DA_V_DIM, 1), B, S, tq=512)
    ohg = _hgrn(hq, hf, hi, og, hg_lower_bound, hg_norm[0].reshape(1, HG_V), B, S, tc=256)

    out = _post(x2, oda, ohg, gates,
                w_da_out[0].astype(BF16), w_hg_out[0].astype(BF16), w_out[0].astype(BF16),
                norm_ffn[0].reshape(1, D), w_ffn_in[0].astype(BF16), w_ffn_out[0].astype(BF16),
                norm_final.reshape(1, D), tm=256, fc=256)
    return out.reshape(B, S, D)
```

```python
import functools
import math

import jax
import jax.numpy as jnp
from jax import lax
from jax.experimental import pallas as pl
from jax.experimental.pallas import tpu as pltpu

F32 = jnp.float32
BF16 = jnp.bfloat16

D_MODEL = 1024
CHUNK = 64
EPS = 1e-6
DA_HEADS = 4
DA_QK_DIM = 64
DA_V_DIM = 128
DA_WIDTH = DA_HEADS * DA_V_DIM
HG_HEADS = 4
HG_K = 128
HG_V = 128
HG_WIDTH = HG_HEADS * HG_V
D_FF = 2816
LAM_INIT = 0.8 - 0.6 * math.exp(-0.3 * 0)

C_Q, C_K, C_V, C_HQ, C_HF, C_HI, C_OG, C_GDA, C_GHG, C_END = (
    0, 512, 1024, 1536, 2048, 2560, 3072, 3584, 4608, 5632)

VMEM_LIMIT = 56 * 1024 * 1024
NEG = float(jnp.finfo(jnp.float32).min)

_NT = (((1,), (1,)), ((), ()))
_TN = (((0,), (0,)), ((), ()))


def _const_spec(shape):
    nd = len(shape)
    return pl.BlockSpec(shape, lambda *_: (0,) * nd, pipeline_mode=pl.Buffered(1))


def _inproj_kernel(x_ref, g_ref, w_ref, wvt_ref,
                   q_ref, k_ref, vt_ref, hq_ref, hf_ref, hi_ref, og_ref, gates_ref):
    x = x_ref[...]
    ms = jnp.mean(x * x, axis=-1, keepdims=True)
    u = (x * lax.rsqrt(ms + EPS) * g_ref[...]).astype(BF16)

    def seg(c0, c1):
        return jnp.dot(u, w_ref[:, c0:c1], preferred_element_type=F32)

    q_ref[...] = (seg(C_Q, C_K) * (DA_QK_DIM ** -0.5 * math.log2(math.e))).astype(BF16)
    k_ref[...] = seg(C_K, C_V).astype(BF16)
    vt_ref[0] = lax.dot_general(wvt_ref[...], u, _NT,
                                preferred_element_type=F32).astype(BF16)
    hq_ref[...] = seg(C_HQ, C_HF).astype(BF16)
    hf_ref[...] = seg(C_HF, C_HI)
    hi_ref[...] = seg(C_HI, C_OG).astype(BF16)
    og_ref[...] = seg(C_OG, C_GDA).astype(BF16)
    gates_ref[:, 0:1024] = seg(C_GDA, C_GHG).astype(BF16)
    gates_ref[:, 1024:2048] = seg(C_GHG, C_END).astype(BF16)


def _inproj(x2, g, w, wvt, B, S, tm):
    T = B * S
    ns = S // tm
    row = lambda b, s: (b * ns + s, 0)
    tok = lambda n, dt: jax.ShapeDtypeStruct((T, n), dt)
    return pl.pallas_call(
        _inproj_kernel,
        grid=(B, ns),
        in_specs=[
            pl.BlockSpec((tm, D_MODEL), row),
            _const_spec((1, D_MODEL)),
            _const_spec((D_MODEL, C_END)),
            _const_spec((DA_WIDTH, D_MODEL)),
        ],
        out_specs=[
            pl.BlockSpec((tm, 512), row),
            pl.BlockSpec((tm, 512), row),
            pl.BlockSpec((1, DA_WIDTH, tm), lambda b, s: (b, 0, s)),
            pl.BlockSpec((tm, 512), row),
            pl.BlockSpec((tm, 512), row),
            pl.BlockSpec((tm, 512), row),
            pl.BlockSpec((tm, 512), row),
            pl.BlockSpec((tm, 2048), row),
        ],
        out_shape=[
            tok(512, BF16), tok(512, BF16),
            jax.ShapeDtypeStruct((B, DA_WIDTH, S), BF16),
            tok(512, BF16), tok(512, F32), tok(512, BF16), tok(512, BF16),
            tok(2048, BF16),
        ],
        compiler_params=pltpu.CompilerParams(
            dimension_semantics=("arbitrary", "arbitrary"),
            vmem_limit_bytes=VMEM_LIMIT),
        name="inproj",
    )(x2, g, w, wvt)


def _attn_kernel(q_ref, k_ref, vt_ref, lam_ref, subln_ref, o_ref,
                 s_scr, mt_scr, p_scr, al_scr, m_scr, l_scr, acc_scr, *, tq):
    qi = pl.program_id(2)
    q = q_ref[...]
    lane = lax.broadcasted_iota(jnp.int32, q.shape, 1)
    zero = jnp.zeros_like(q)
    qs = (jnp.where(lane < DA_QK_DIM, q, zero), jnp.where(lane >= DA_QK_DIM, q, zero))

    m_scr[...] = jnp.full(m_scr.shape, NEG, F32)
    l_scr[...] = jnp.zeros(l_scr.shape, F32)
    acc_scr[...] = jnp.zeros(acc_scr.shape, F32)

    def scores(kt, masked):
        start = pl.multiple_of(kt * tq, tq)
        k = k_ref[pl.ds(start, tq), :]
        if masked:
            krow = lax.broadcasted_iota(jnp.int32, (tq, tq), 0) // CHUNK
            qcol = lax.broadcasted_iota(jnp.int32, (tq, tq), 1) // CHUNK
            vis = krow <= qcol
        for i in range(2):
            s = lax.dot_general(k, qs[i], _NT, preferred_element_type=F32)
            if masked:
                s = jnp.where(vis, s, NEG)
            s_scr[i] = s
            mt_scr[i] = jnp.max(s, axis=0, keepdims=True)

    def softmax_update():
        for i in range(2):
            m_old = m_scr[i]
            m_new = jnp.maximum(m_old, mt_scr[i])
            alpha = jnp.exp2(m_old - m_new)
            p = jnp.exp2(s_scr[i] - m_new)
            l_scr[i] = alpha * l_scr[i] + jnp.sum(p, axis=0, keepdims=True)
            p_scr[i] = p.astype(BF16)
            al_scr[i] = alpha
            m_scr[i] = m_new

    def accumulate(kt):
        start = pl.multiple_of(kt * tq, tq)
        vt = vt_ref[0, :, pl.ds(start, tq)]
        for i in range(2):
            pv = jnp.dot(vt, p_scr[i], preferred_element_type=F32)
            acc_scr[i] = al_scr[i] * acc_scr[i] + pv

    scores(qi, True)
    softmax_update()
    scores(0, False)

    def body(kt, c):
        accumulate(jnp.where(kt == 0, qi, kt - 1))
        softmax_update()
        scores(jnp.minimum(kt + 1, qi - 1), False)
        return c

    lax.fori_loop(0, qi, body, 0)
    accumulate(jnp.maximum(qi - 1, 0))

    lp = lam_ref[...]
    lam = (jnp.exp(jnp.sum(lp[0:1] * lp[1:2], axis=1, keepdims=True))
           - jnp.exp(jnp.sum(lp[2:3] * lp[3:4], axis=1, keepdims=True)) + LAM_INIT)
    o = acc_scr[0] * (1.0 / l_scr[0]) - lam * (acc_scr[1] * (1.0 / l_scr[1]))
    ms = jnp.mean(o * o, axis=0, keepdims=True)
    y = o * lax.rsqrt(ms + EPS) * subln_ref[...] * (1.0 - LAM_INIT)
    o_ref[...] = y.T.astype(BF16)


def _attention(q, k, vt, lam, subln_col, B, S, tq):
    T = B * S
    nq = S // tq
    kern = functools.partial(_attn_kernel, tq=tq)
    return pl.pallas_call(
        kern,
        grid=(B, DA_HEADS, nq),
        in_specs=[
            pl.BlockSpec((tq, 128), lambda b, h, i: (b * nq + i, h)),
            pl.BlockSpec((S, 128), lambda b, h, i: (b, h)),
            pl.BlockSpec((1, 128, S), lambda b, h, i: (b, h, 0)),
            _const_spec((4, DA_QK_DIM)),
            _const_spec((DA_V_DIM, 1)),
        ],
        out_specs=pl.BlockSpec((tq, 128), lambda b, h, i: (b * nq + i, h)),
        out_shape=jax.ShapeDtypeStruct((T, DA_WIDTH), BF16),
        scratch_shapes=[
            pltpu.VMEM((2, tq, tq), F32),
            pltpu.VMEM((2, 1, tq), F32),
            pltpu.VMEM((2, tq, tq), BF16),
            pltpu.VMEM((2, 1, tq), F32),
            pltpu.VMEM((2, 1, tq), F32),
            pltpu.VMEM((2, 1, tq), F32),
            pltpu.VMEM((2, DA_V_DIM, tq), F32),
        ],
        compiler_params=pltpu.CompilerParams(
            dimension_semantics=("arbitrary", "arbitrary", "arbitrary"),
            vmem_limit_bytes=VMEM_LIMIT),
        name="diffattn",
    )(q, k, vt, lam, subln_col)


def _split3(a):
    hi = a.astype(BF16)
    r = a - hi.astype(F32)
    mid = r.astype(BF16)
    lo = (r - mid.astype(F32)).astype(BF16)
    return hi, mid, lo


def _hgrn_kernel(hq_ref, hf_ref, hi_ref, og_ref, lbraw_ref, gn_ref, o_ref, st_scr, *, tc):
    @pl.when(pl.program_id(1) == 0)
    def _():
        st_scr[...] = jnp.zeros(st_scr.shape, F32)

    nchunk = tc // CHUNK
    lbr = lbraw_ref[...]
    e = jnp.exp(lbr - jnp.max(lbr, axis=0, keepdims=True))
    lb = e[0:1] / jnp.sum(e, axis=0, keepdims=True)

    hq = hq_ref[...].astype(F32)
    q = hq * jax.nn.sigmoid(hq)
    f = lb + (1.0 - lb) * jax.nn.sigmoid(hf_ref[...])
    logf = jnp.log(f)
    kk = 1.0 - f
    v = hi_ref[...]

    r = lax.broadcasted_iota(jnp.int32, (tc, tc), 0)
    c = lax.broadcasted_iota(jnp.int32, (tc, tc), 1)
    tril = (r >= c) & ((r // CHUNK) == (c // CHUNK))
    ltri = tril.astype(BF16)
    b = sum(jnp.dot(ltri, part, preferred_element_type=F32) for part in _split3(logf))

    qd = (q * jnp.exp(b)).astype(BF16)
    kd = (kk * jnp.exp(-b)).astype(BF16)

    og = og_ref[...].astype(F32)
    gate = og * jax.nn.sigmoid(og)
    gn = gn_ref[...]

    for h in range(HG_HEADS):
        hs = slice(h * HG_K, (h + 1) * HG_K)
        a = lax.dot_general(qd[:, hs], kd[:, hs], _NT, preferred_element_type=F32)
        a = jnp.where(tril, a, 0.0).astype(BF16)
        o_intra = jnp.dot(a, v[:, hs], preferred_element_type=F32)
        for n in range(nchunk):
            rs = slice(n * CHUNK, (n + 1) * CHUNK)
            b_c = b[rs, hs]
            b_last = b_c[CHUNK - 1:CHUNK, :]
            st = st_scr[h]
            o_inter = lax.dot_general(qd[rs, hs], st.astype(BF16), _NT,
                                      preferred_element_type=F32)
            o = o_intra[rs, :] + o_inter
            k_end = (kk[rs, hs] * jnp.exp(b_last - b_c)).astype(BF16)
            d_t = lax.dot_general(v[rs, hs], k_end, _TN, preferred_element_type=F32)
            st_scr[h] = st * jnp.exp(b_last) + d_t
            ms = jnp.mean(o * o, axis=-1, keepdims=True)
            y = o * lax.rsqrt(ms + EPS) * gn * gate[rs, hs]
            o_ref[rs, hs] = y.astype(BF16)


def _hgrn(hq, hf, hi, og, lbraw, gn, B, S, tc):
    T = B * S
    ns = S // tc
    row = lambda b, s: (b * ns + s, 0)
    kern = functools.partial(_hgrn_kernel, tc=tc)
    return pl.pallas_call(
        kern,
        grid=(B, ns),
        in_specs=[
            pl.BlockSpec((tc, 512), row),
            pl.BlockSpec((tc, 512), row),
            pl.BlockSpec((tc, 512), row),
            pl.BlockSpec((tc, 512), row),
            _const_spec((2, HG_WIDTH)),
            _const_spec((1, HG_V)),
        ],
        out_specs=pl.BlockSpec((tc, 512), row),
        out_shape=jax.ShapeDtypeStruct((T, HG_WIDTH), BF16),
        scratch_shapes=[pltpu.VMEM((HG_HEADS, HG_V, HG_K), F32)],
        compiler_params=pltpu.CompilerParams(
            dimension_semantics=("arbitrary", "arbitrary"),
            vmem_limit_bytes=VMEM_LIMIT),
        name="hgrn2",
    )(hq, hf, hi, og, lbraw, gn)


def _rms(x, g):
    ms = jnp.mean(x * x, axis=-1, keepdims=True)
    return x * lax.rsqrt(ms + EPS) * g


def _post_kernel(x_ref, oda_ref, ohg_ref, gates_ref, wda_ref, whg_ref, wout_ref,
                 nffn_ref, wfi_ref, wfo_ref, nfin_ref, out_ref, act_scr, *, fc):
    y_da = jnp.dot(oda_ref[...], wda_ref[...], preferred_element_type=F32)
    y_hg = jnp.dot(ohg_ref[...], whg_ref[...], preferred_element_type=F32)
    g_da = jax.nn.sigmoid(gates_ref[:, 0:1024].astype(F32))
    g_hg = jax.nn.sigmoid(gates_ref[:, 1024:2048].astype(F32))
    merged = (g_da * y_da + g_hg * y_hg).astype(BF16)
    h = x_ref[...] + jnp.dot(merged, wout_ref[...], preferred_element_type=F32)
    z = _rms(h, nffn_ref[...]).astype(BF16)
    for c0 in range(0, D_FF, fc):
        gate = jnp.dot(z, wfi_ref[:, c0:c0 + fc], preferred_element_type=F32)
        up = jnp.dot(z, wfi_ref[:, D_FF + c0:D_FF + c0 + fc], preferred_element_type=F32)
        act_scr[:, c0:c0 + fc] = (gate * jax.nn.sigmoid(gate) * up).astype(BF16)
    h = h + jnp.dot(act_scr[...], wfo_ref[...], preferred_element_type=F32)
    out_ref[...] = _rms(h, nfin_ref[...])


def _post(x2, oda, ohg, gates, wda, whg, wout, nffn, wfi, wfo, nfin, tm, fc):
    T = x2.shape[0]
    row = lambda i: (i, 0)
    kern = functools.partial(_post_kernel, fc=fc)
    return pl.pallas_call(
        kern,
        grid=(T // tm,),
        in_specs=[
            pl.BlockSpec((tm, D_MODEL), row),
            pl.BlockSpec((tm, DA_WIDTH), row),
            pl.BlockSpec((tm, HG_WIDTH), row),
            pl.BlockSpec((tm, 2048), row),
            _const_spec((DA_WIDTH, D_MODEL)),
            _const_spec((HG_WIDTH, D_MODEL)),
            _const_spec((D_MODEL, D_MODEL)),
            _const_spec((1, D_MODEL)),
            _const_spec((D_MODEL, 2 * D_FF)),
            _const_spec((D_FF, D_MODEL)),
            _const_spec((1, D_MODEL)),
        ],
        out_specs=pl.BlockSpec((tm, D_MODEL), row),
        out_shape=jax.ShapeDtypeStruct((T, D_MODEL), F32),
        scratch_shapes=[pltpu.VMEM((tm, D_FF), BF16)],
        compiler_params=pltpu.CompilerParams(
            dimension_semantics=("arbitrary",),
            vmem_limit_bytes=VMEM_LIMIT),
        name="post",
    )(x2, oda, ohg, gates, wda, whg, wout, nffn, wfi, wfo, nfin)


def kernel(x, norm_mix, w_in, da_lambda, da_subln, hg_lower_bound, hg_norm,
           w_da_out, w_hg_out, w_out, norm_ffn, w_ffn_in, w_ffn_out, norm_final):
    B, S, D = x.shape
    T = B * S
    x2 = x.reshape(T, D)
    w_in_b = w_in[0].astype(BF16)
    wvt = w_in[0][:, C_V:C_HQ].T.astype(BF16)

    q, k, vt, hq, hf, hi, og, gates = _inproj(
        x2, norm_mix[0].reshape(1, D), w_in_b, wvt, B, S, tm=512)

    oda = _attention(q, k, vt, da_lambda[0], da_subln[0].reshape(DA_V_DIM, 1), B, S, tq=512)
    ohg = _hgrn(hq, hf, hi, og, hg_lower_bound, hg_norm[0].reshape(1, HG_V), B, S, tc=256)

    out = _post(x2, oda, ohg, gates,
                w_da_out[0].astype(BF16), w_hg_out[0].astype(BF16), w_out[0].astype(BF16),
                norm_ffn[0].reshape(1, D), w_ffn_in[0].astype(BF16), w_ffn_out[0].astype(BF16),
                norm_final.reshape(1, D), tm=256, fc=256)
    return out.reshape(B, S, D)
```

```python
import functools
import math

import jax
import jax.numpy as jnp
from jax import lax
from jax.experimental import pallas as pl
from jax.experimental.pallas import tpu as pltpu

F32 = jnp.float32
BF16 = jnp.bfloat16

D_MODEL = 1024
CHUNK = 64
EPS = 1e-6
DA_HEADS = 4
DA_QK_DIM = 64
DA_V_DIM = 128
DA_WIDTH = DA_HEADS * DA_V_DIM
HG_HEADS = 4
HG_K = 128
HG_V = 128
HG_WIDTH = HG_HEADS * HG_V
D_FF = 2816
LAM_INIT = 0.8 - 0.6 * math.exp(-0.3 * 0)

C_Q, C_K, C_V, C_HQ, C_HF, C_HI, C_OG, C_GDA, C_GHG, C_END = (
    0, 512, 1024, 1536, 2048, 2560, 3072, 3584, 4608, 5632)

VMEM_LIMIT = 56 * 1024 * 1024
NEG = float(jnp.finfo(jnp.float32).min)

_NT = (((1,), (1,)), ((), ()))
_TN = (((0,), (0,)), ((), ()))


def _const_spec(shape):
    nd = len(shape)
    return pl.BlockSpec(shape, lambda *_: (0,) * nd, pipeline_mode=pl.Buffered(1))


def _inproj_kernel(x_ref, g_ref, w_ref, wvt_ref,
                   q_ref, k_ref, vt_ref, hq_ref, hf_ref, hi_ref, og_ref, gates_ref):
    x = x_ref[...]
    ms = jnp.mean(x * x, axis=-1, keepdims=True)
    u = (x * lax.rsqrt(ms + EPS) * g_ref[...]).astype(BF16)

    def seg(c0, c1):
        return jnp.dot(u, w_ref[:, c0:c1], preferred_element_type=F32)

    q_ref[...] = (seg(C_Q, C_K) * (DA_QK_DIM ** -0.5 * math.log2(math.e))).astype(BF16)
    k_ref[...] = seg(C_K, C_V).astype(BF16)
    vt_ref[0] = lax.dot_general(wvt_ref[...], u, _NT,
                                preferred_element_type=F32).astype(BF16)
    hq_ref[...] = seg(C_HQ, C_HF).astype(BF16)
    hf_ref[...] = seg(C_HF, C_HI)
    hi_ref[...] = seg(C_HI, C_OG).astype(BF16)
    og_ref[...] = seg(C_OG, C_GDA).astype(BF16)
    gates_ref[:, 0:1024] = seg(C_GDA, C_GHG).astype(BF16)
    gates_ref[:, 1024:2048] = seg(C_GHG, C_END).astype(BF16)


def _inproj(x2, g, w, wvt, B, S, tm):
    T = B * S
    ns = S // tm
    row = lambda b, s: (b * ns + s, 0)
    tok = lambda n, dt: jax.ShapeDtypeStruct((T, n), dt)
    return pl.pallas_call(
        _inproj_kernel,
        grid=(B, ns),
        in_specs=[
            pl.BlockSpec((tm, D_MODEL), row),
            _const_spec((1, D_MODEL)),
            _const_spec((D_MODEL, C_END)),
            _const_spec((DA_WIDTH, D_MODEL)),
        ],
        out_specs=[
            pl.BlockSpec((tm, 512), row),
            pl.BlockSpec((tm, 512), row),
            pl.BlockSpec((1, DA_WIDTH, tm), lambda b, s: (b, 0, s)),
            pl.BlockSpec((tm, 512), row),
            pl.BlockSpec((tm, 512), row),
            pl.BlockSpec((tm, 512), row),
            pl.BlockSpec((tm, 512), row),
            pl.BlockSpec((tm, 2048), row),
        ],
        out_shape=[
            tok(512, BF16), tok(512, BF16),
            jax.ShapeDtypeStruct((B, DA_WIDTH, S), BF16),
            tok(512, BF16), tok(512, F32), tok(512, BF16), tok(512, BF16),
            tok(2048, BF16),
        ],
        compiler_params=pltpu.CompilerParams(
            dimension_semantics=("arbitrary", "arbitrary"),
            vmem_limit_bytes=VMEM_LIMIT),
        name="inproj",
    )(x2, g, w, wvt)


def _attn_kernel(qa_ref, qb_ref, k_ref, vt_ref, lam_ref, subln_ref, o_ref,
                 q_scr, s_scr, mt_scr, p_scr, al_scr, m_scr, l_scr, acc_scr, *, tq, nq, unroll):
    i = pl.program_id(2)
    for slot, ref in enumerate((qa_ref, qb_ref)):
        q = ref[...]
        lane = lax.broadcasted_iota(jnp.int32, q.shape, 1)
        zero = jnp.zeros_like(q)
        q_scr[slot, 0] = jnp.where(lane < DA_QK_DIM, q, zero)
        q_scr[slot, 1] = jnp.where(lane >= DA_QK_DIM, q, zero)

    m_scr[...] = jnp.full(m_scr.shape, NEG, F32)
    l_scr[...] = jnp.zeros(l_scr.shape, F32)
    acc_scr[...] = jnp.zeros(acc_scr.shape, F32)

    def tile_of(j):
        if isinstance(j, int) and j == 0:
            return jnp.minimum(i, 0), i
        if isinstance(j, int) and j == nq:
            return 1 - jnp.minimum(i, 0), nq - 1 - i
        in_a = j <= i
        return (jnp.where(in_a, 0, 1),
                jnp.where(in_a, jnp.where(j == 0, i, j - 1), j - 1 - i))

    def scores(j, masked=False):
        slot, kt = tile_of(j)
        start = pl.multiple_of(kt * tq, tq)
        k = k_ref[pl.ds(start, tq), :]
        if masked:
            qchunk = lax.broadcasted_iota(jnp.int32, (1, tq), 1) // CHUNK
        for mp in range(2):
            s = lax.dot_general(k, q_scr[slot, mp], _NT, preferred_element_type=F32)
            if masked:
                s = jnp.concatenate(
                    [jnp.where(qchunk >= c, s[c * CHUNK:(c + 1) * CHUNK, :], NEG)
                     for c in range(tq // CHUNK)], axis=0)
            s_scr[mp] = s
            mt_scr[mp] = jnp.max(s, axis=0, keepdims=True)

    def softmax_update(j):
        slot, _ = tile_of(j)
        for mp in range(2):
            m_old = m_scr[slot, mp]
            m_new = jnp.maximum(m_old, mt_scr[mp])
            alpha = jnp.exp2(m_old - m_new)
            p = jnp.exp2(s_scr[mp] - m_new)
            l_scr[slot, mp] = alpha * l_scr[slot, mp] + jnp.sum(p, axis=0, keepdims=True)
            p_scr[mp] = p.astype(BF16)
            al_scr[mp] = alpha
            m_scr[slot, mp] = m_new

    def accumulate(j):
        slot, kt = tile_of(j)
        start = pl.multiple_of(kt * tq, tq)
        vt = vt_ref[0, :, pl.ds(start, tq)]
        for mp in range(2):
            pv = jnp.dot(vt, p_scr[mp], preferred_element_type=F32)
            acc_scr[slot, mp] = al_scr[mp] * acc_scr[slot, mp] + pv

    def step(j, last_scores_masked=False):
        accumulate(j - 1)
        softmax_update(j)
        scores(j + 1, masked=last_scores_masked)

    scores(0, masked=True)
    softmax_update(0)
    scores(1)

    def body(t, c):
        for u in range(unroll):
            step(1 + t * unroll + u)
        return c

    lax.fori_loop(0, (nq - 2) // unroll, body, 0)
    step(nq - 1, last_scores_masked=True)

    @pl.when(i >= 0)
    def _():
        accumulate(nq - 1)
        softmax_update(nq)

    @pl.when(i >= 0)
    def _():
        accumulate(nq)

    lp = lam_ref[...]
    lam = (jnp.exp(jnp.sum(lp[0:1] * lp[1:2], axis=1, keepdims=True))
           - jnp.exp(jnp.sum(lp[2:3] * lp[3:4], axis=1, keepdims=True)) + LAM_INIT)
    for slot in range(2):
        o = (acc_scr[slot, 0] * (1.0 / l_scr[slot, 0])
             - lam * (acc_scr[slot, 1] * (1.0 / l_scr[slot, 1])))
        ms = jnp.mean(o * o, axis=0, keepdims=True)
        y = o * lax.rsqrt(ms + EPS) * subln_ref[...] * (1.0 - LAM_INIT)
        o_ref[0, slot, 0] = y.T.astype(BF16)


def _attention(q, k, vt, lam, subln_col, B, S, tq, unroll):
    nq = S // tq
    nh = nq // 2
    assert nq % 2 == 0 and (nq - 2) % unroll == 0
    kern = functools.partial(_attn_kernel, tq=tq, nq=nq, unroll=unroll)
    return pl.pallas_call(
        kern,
        grid=(B, DA_HEADS, nh),
        in_specs=[
            pl.BlockSpec((tq, 128), lambda b, h, i: (b * nq + i, h)),
            pl.BlockSpec((tq, 128), lambda b, h, i: (b * nq + nq - 1 - i, h)),
            pl.BlockSpec((S, 128), lambda b, h, i: (b, h)),
            pl.BlockSpec((1, 128, S), lambda b, h, i: (b, h, 0)),
            _const_spec((4, DA_QK_DIM)),
            _const_spec((DA_V_DIM, 1)),
        ],
        out_specs=pl.BlockSpec((1, 2, 1, tq, 128), lambda b, h, i: (b, 0, i, 0, h)),
        out_shape=jax.ShapeDtypeStruct((B, 2, nh, tq, DA_WIDTH), BF16),
        scratch_shapes=[
            pltpu.VMEM((2, 2, tq, 128), BF16),
            pltpu.VMEM((2, tq, tq), F32),
            pltpu.VMEM((2, 1, tq), F32),
            pltpu.VMEM((2, tq, tq), BF16),
            pltpu.VMEM((2, 1, tq), F32),
            pltpu.VMEM((2, 2, 1, tq), F32),
            pltpu.VMEM((2, 2, 1, tq), F32),
            pltpu.VMEM((2, 2, DA_V_DIM, tq), F32),
        ],
        compiler_params=pltpu.CompilerParams(
            dimension_semantics=("arbitrary", "arbitrary", "arbitrary"),
            vmem_limit_bytes=VMEM_LIMIT),
        name="diffattn",
    )(q, q, k, vt, lam, subln_col)


def _split3(a):
    hi = a.astype(BF16)
    r = a - hi.astype(F32)
    mid = r.astype(BF16)
    lo = (r - mid.astype(F32)).astype(BF16)
    return hi, mid, lo


def _hgrn_kernel(hq_ref, hf_ref, hi_ref, og_ref, lbraw_ref, gn_ref, o_ref, st_scr, *, tc):
    @pl.when(pl.program_id(1) == 0)
    def _():
        st_scr[...] = jnp.zeros(st_scr.shape, F32)

    nchunk = tc // CHUNK
    lbr = lbraw_ref[...]
    e = jnp.exp(lbr - jnp.max(lbr, axis=0, keepdims=True))
    lb = e[0:1] / jnp.sum(e, axis=0, keepdims=True)

    hq = hq_ref[...].astype(F32)
    q = hq * jax.nn.sigmoid(hq)
    f = lb + (1.0 - lb) * jax.nn.sigmoid(hf_ref[...])
    logf = jnp.log(f)
    kk = 1.0 - f
    v = hi_ref[...]

    r = lax.broadcasted_iota(jnp.int32, (tc, tc), 0)
    c = lax.broadcasted_iota(jnp.int32, (tc, tc), 1)
    tril = (r >= c) & ((r // CHUNK) == (c // CHUNK))
    ltri = tril.astype(BF16)
    b = sum(jnp.dot(ltri, part, preferred_element_type=F32) for part in _split3(logf))

    qd = (q * jnp.exp(b)).astype(BF16)
    kd = (kk * jnp.exp(-b)).astype(BF16)

    og = og_ref[...].astype(F32)
    gate = og * jax.nn.sigmoid(og)
    gn = gn_ref[...]

    for h in range(HG_HEADS):
        hs = slice(h * HG_K, (h + 1) * HG_K)
        a = lax.dot_general(qd[:, hs], kd[:, hs], _NT, preferred_element_type=F32)
        a = jnp.where(tril, a, 0.0).astype(BF16)
        o_intra = jnp.dot(a, v[:, hs], preferred_element_type=F32)
        for n in range(nchunk):
            rs = slice(n * CHUNK, (n + 1) * CHUNK)
            b_c = b[rs, hs]
            b_last = b_c[CHUNK - 1:CHUNK, :]
            st = st_scr[h]
            o_inter = lax.dot_general(qd[rs, hs], st.astype(BF16), _NT,
                                      preferred_element_type=F32)
            o = o_intra[rs, :] + o_inter
            k_end = (kk[rs, hs] * jnp.exp(b_last - b_c)).astype(BF16)
            d_t = lax.dot_general(v[rs, hs], k_end, _TN, preferred_element_type=F32)
            st_scr[h] = st * jnp.exp(b_last) + d_t
            ms = jnp.mean(o * o, axis=-1, keepdims=True)
            y = o * lax.rsqrt(ms + EPS) * gn * gate[rs, hs]
            o_ref[rs, hs] = y.astype(BF16)


def _hgrn(hq, hf, hi, og, lbraw, gn, B, S, tc):
    T = B * S
    ns = S // tc
    row = lambda b, s: (b * ns + s, 0)
    kern = functools.partial(_hgrn_kernel, tc=tc)
    return pl.pallas_call(
        kern,
        grid=(B, ns),
        in_specs=[
            pl.BlockSpec((tc, 512), row),
            pl.BlockSpec((tc, 512), row),
            pl.BlockSpec((tc, 512), row),
            pl.BlockSpec((tc, 512), row),
            _const_spec((2, HG_WIDTH)),
            _const_spec((1, HG_V)),
        ],
        out_specs=pl.BlockSpec((tc, 512), row),
        out_shape=jax.ShapeDtypeStruct((T, HG_WIDTH), BF16),
        scratch_shapes=[pltpu.VMEM((HG_HEADS, HG_V, HG_K), F32)],
        compiler_params=pltpu.CompilerParams(
            dimension_semantics=("arbitrary", "arbitrary"),
            vmem_limit_bytes=VMEM_LIMIT),
        name="hgrn2",
    )(hq, hf, hi, og, lbraw, gn)


def _rms(x, g):
    ms = jnp.mean(x * x, axis=-1, keepdims=True)
    return x * lax.rsqrt(ms + EPS) * g


def _post_kernel(x_ref, oda_ref, ohg_ref, gates_ref, wda_ref, whg_ref, wout_ref,
                 nffn_ref, wfi_ref, wfo_ref, nfin_ref, out_ref, act_scr, *, fc):
    y_da = jnp.dot(oda_ref[...], wda_ref[...], preferred_element_type=F32)
    y_hg = jnp.dot(ohg_ref[...], whg_ref[...], preferred_element_type=F32)
    g_da = jax.nn.sigmoid(gates_ref[:, 0:1024].astype(F32))
    g_hg = jax.nn.sigmoid(gates_ref[:, 1024:2048].astype(F32))
    merged = (g_da * y_da + g_hg * y_hg).astype(BF16)
    h = x_ref[...] + jnp.dot(merged, wout_ref[...], preferred_element_type=F32)
    z = _rms(h, nffn_ref[...]).astype(BF16)
    for c0 in range(0, D_FF, fc):
        gate = jnp.dot(z, wfi_ref[:, c0:c0 + fc], preferred_element_type=F32)
        up = jnp.dot(z, wfi_ref[:, D_FF + c0:D_FF + c0 + fc], preferred_element_type=F32)
        act_scr[:, c0:c0 + fc] = (gate * jax.nn.sigmoid(gate) * up).astype(BF16)
    h = h + jnp.dot(act_scr[...], wfo_ref[...], preferred_element_type=F32)
    out_ref[...] = _rms(h, nfin_ref[...])


def _post(x2, oda, ohg, gates, wda, whg, wout, nffn, wfi, wfo, nfin, tm, fc):
    T = x2.shape[0]
    row = lambda i: (i, 0)
    B, _, nh, tq, _ = oda.shape
    sub = tq // tm
    oda = oda.reshape(B, 2, nh, sub, tm, DA_WIDTH)
    tiles_per_batch = 2 * nh * sub

    def oda_idx(i):
        b = i // tiles_per_batch
        r = i % tiles_per_batch
        qt = r // sub
        upper = qt >= nh
        return (b, jnp.where(upper, 1, 0), jnp.where(upper, 2 * nh - 1 - qt, qt), r % sub, 0, 0)

    kern = functools.partial(_post_kernel, fc=fc)
    return pl.pallas_call(
        kern,
        grid=(T // tm,),
        in_specs=[
            pl.BlockSpec((tm, D_MODEL), row),
            pl.BlockSpec((None, None, None, None, tm, DA_WIDTH), oda_idx),
            pl.BlockSpec((tm, HG_WIDTH), row),
            pl.BlockSpec((tm, 2048), row),
            _const_spec((DA_WIDTH, D_MODEL)),
            _const_spec((HG_WIDTH, D_MODEL)),
            _const_spec((D_MODEL, D_MODEL)),
            _const_spec((1, D_MODEL)),
            _const_spec((D_MODEL, 2 * D_FF)),
            _const_spec((D_FF, D_MODEL)),
            _const_spec((1, D_MODEL)),
        ],
        out_specs=pl.BlockSpec((tm, D_MODEL), row),
        out_shape=jax.ShapeDtypeStruct((T, D_MODEL), F32),
        scratch_shapes=[pltpu.VMEM((tm, D_FF), BF16)],
        compiler_params=pltpu.CompilerParams(
            dimension_semantics=("arbitrary",),
            vmem_limit_bytes=VMEM_LIMIT),
        name="post",
    )(x2, oda, ohg, gates, wda, whg, wout, nffn, wfi, wfo, nfin)


def kernel(x, norm_mix, w_in, da_lambda, da_subln, hg_lower_bound, hg_norm,
           w_da_out, w_hg_out, w_out, norm_ffn, w_ffn_in, w_ffn_out, norm_final):
    B, S, D = x.shape
    T = B * S
    x2 = x.reshape(T, D)
    w_in_b = w_in[0].astype(BF16)
    wvt = w_in[0][:, C_V:C_HQ].T.astype(BF16)

    q, k, vt, hq, hf, hi, og, gates = _inproj(
        x2, norm_mix[0].reshape(1, D), w_in_b, wvt, B, S, tm=512)

    oda = _attention(q, k, vt, da_lambda[0], da_subln[0].reshape(DA_V_DIM, 1), B, S, tq=512, unroll=2)
    ohg = _hgrn(hq, hf, hi, og, hg_lower_bound, hg_norm[0].reshape(1, HG_V), B, S, tc=256)

    out = _post(x2, oda, ohg, gates,
                w_da_out[0].astype(BF16), w_hg_out[0].astype(BF16), w_out[0].astype(BF16),
                norm_ffn[0].reshape(1, D), w_ffn_in[0].astype(BF16), w_ffn_out[0].astype(BF16),
                norm_final.reshape(1, D), tm=256, fc=256)
    return out.reshape(B, S, D)
```

```python
import functools
import math

import jax
import jax.numpy as jnp
from jax import lax
from jax.experimental import pallas as pl
from jax.experimental.pallas import tpu as pltpu

F32 = jnp.float32
BF16 = jnp.bfloat16

D_MODEL = 1024
CHUNK = 64
EPS = 1e-6
DA_HEADS = 4
DA_QK_DIM = 64
DA_V_DIM = 128
DA_WIDTH = DA_HEADS * DA_V_DIM
HG_HEADS = 4
HG_K = 128
HG_V = 128
HG_WIDTH = HG_HEADS * HG_V
D_FF = 2816
LAM_INIT = 0.8 - 0.6 * math.exp(-0.3 * 0)

C_Q, C_K, C_V, C_HQ, C_HF, C_HI, C_OG, C_GDA, C_GHG, C_END = (
    0, 512, 1024, 1536, 2048, 2560, 3072, 3584, 4608, 5632)

VMEM_LIMIT = 56 * 1024 * 1024
NEG = float(jnp.finfo(jnp.float32).min)

_NT = (((1,), (1,)), ((), ()))
_TN = (((0,), (0,)), ((), ()))


def _const_spec(shape):
    nd = len(shape)
    return pl.BlockSpec(shape, lambda *_: (0,) * nd, pipeline_mode=pl.Buffered(1))


def _inproj_kernel(x_ref, g_ref, w_ref, wvt_ref,
                   q_ref, k_ref, vt_ref, hq_ref, hf_ref, hi_ref, og_ref, gates_ref):
    x = x_ref[...]
    ms = jnp.mean(x * x, axis=-1, keepdims=True)
    u = (x * lax.rsqrt(ms + EPS) * g_ref[...]).astype(BF16)

    def seg(c0, c1):
        return jnp.dot(u, w_ref[:, c0:c1], preferred_element_type=F32)

    q_ref[...] = (seg(C_Q, C_K) * (DA_QK_DIM ** -0.5 * math.log2(math.e))).astype(BF16)
    k_ref[...] = seg(C_K, C_V).astype(BF16)
    vt_ref[0] = lax.dot_general(wvt_ref[...], u, _NT,
                                preferred_element_type=F32).astype(BF16)
    hq_ref[...] = seg(C_HQ, C_HF).astype(BF16)
    hf_ref[...] = seg(C_HF, C_HI)
    hi_ref[...] = seg(C_HI, C_OG).astype(BF16)
    og_ref[...] = seg(C_OG, C_GDA).astype(BF16)
    gates_ref[:, 0:1024] = seg(C_GDA, C_GHG).astype(BF16)
    gates_ref[:, 1024:2048] = seg(C_GHG, C_END).astype(BF16)


def _inproj(x2, g, w, wvt, B, S, tm):
    T = B * S
    ns = S // tm
    row = lambda b, s: (b * ns + s, 0)
    tok = lambda n, dt: jax.ShapeDtypeStruct((T, n), dt)
    return pl.pallas_call(
        _inproj_kernel,
        grid=(B, ns),
        in_specs=[
            pl.BlockSpec((tm, D_MODEL), row),
            _const_spec((1, D_MODEL)),
            _const_spec((D_MODEL, C_END)),
            _const_spec((DA_WIDTH, D_MODEL)),
        ],
        out_specs=[
            pl.BlockSpec((tm, 512), row),
            pl.BlockSpec((tm, 512), row),
            pl.BlockSpec((1, DA_WIDTH, tm), lambda b, s: (b, 0, s)),
            pl.BlockSpec((tm, 512), row),
            pl.BlockSpec((tm, 512), row),
            pl.BlockSpec((tm, 512), row),
            pl.BlockSpec((tm, 512), row),
            pl.BlockSpec((tm, 2048), row),
        ],
        out_shape=[
            tok(512, BF16), tok(512, BF16),
            jax.ShapeDtypeStruct((B, DA_WIDTH, S), BF16),
            tok(512, BF16), tok(512, F32), tok(512, BF16), tok(512, BF16),
            tok(2048, BF16),
        ],
        compiler_params=pltpu.CompilerParams(
            dimension_semantics=("arbitrary", "arbitrary"),
            vmem_limit_bytes=VMEM_LIMIT),
        name="inproj",
    )(x2, g, w, wvt)


def _attn_kernel(qa_ref, qb_ref, k_ref, vt_ref, lam_ref, subln_ref, o_ref,
                 q_scr, s_scr, mt_scr, p_scr, al_scr, m_scr, l_scr, acc_scr, *, tq, nq, unroll):
    i = pl.program_id(2)
    for slot, ref in enumerate((qa_ref, qb_ref)):
        q = ref[...]
        lane = lax.broadcasted_iota(jnp.int32, q.shape, 1)
        zero = jnp.zeros_like(q)
        q_scr[slot, 0] = jnp.where(lane < DA_QK_DIM, q, zero)
        q_scr[slot, 1] = jnp.where(lane >= DA_QK_DIM, q, zero)

    m_scr[...] = jnp.full(m_scr.shape, NEG, F32)
    l_scr[...] = jnp.zeros(l_scr.shape, F32)
    acc_scr[...] = jnp.zeros(acc_scr.shape, F32)

    def tile_of(j):
        if isinstance(j, int) and j == 0:
            return jnp.minimum(i, 0), i
        if isinstance(j, int) and j == nq:
            return 1 - jnp.minimum(i, 0), nq - 1 - i
        in_a = j <= i
        return (jnp.where(in_a, 0, 1),
                jnp.where(in_a, jnp.where(j == 0, i, j - 1), j - 1 - i))

    def scores(j, masked=False):
        slot, kt = tile_of(j)
        start = pl.multiple_of(kt * tq, tq)
        k = k_ref[pl.ds(start, tq), :]
        if masked:
            qchunk = lax.broadcasted_iota(jnp.int32, (1, tq), 1) // CHUNK
        for mp in range(2):
            s = lax.dot_general(k, q_scr[slot, mp], _NT, preferred_element_type=F32)
            if masked:
                s = jnp.concatenate(
                    [jnp.where(qchunk >= c, s[c * CHUNK:(c + 1) * CHUNK, :], NEG)
                     for c in range(tq // CHUNK)], axis=0)
            s_scr[mp, :, 0:tq] = s
            mt_scr[mp] = jnp.max(s, axis=0, keepdims=True)

    def softmax_update(j):
        slot, _ = tile_of(j)
        for mp in range(2):
            m_old = m_scr[slot, mp]
            m_new = jnp.maximum(m_old, mt_scr[mp])
            alpha = jnp.exp2(m_old - m_new)
            p = jnp.exp2(s_scr[mp, :, 0:tq] - m_new)
            l_scr[slot, mp] = alpha * l_scr[slot, mp] + jnp.sum(p, axis=0, keepdims=True)
            p_scr[mp, :, 0:tq] = p.astype(BF16)
            al_scr[mp] = alpha
            m_scr[slot, mp] = m_new

    def accumulate(j):
        slot, kt = tile_of(j)
        start = pl.multiple_of(kt * tq, tq)
        vt = vt_ref[0, :, pl.ds(start, tq)]
        for mp in range(2):
            pv = jnp.dot(vt, p_scr[mp, :, 0:tq], preferred_element_type=F32)
            acc_scr[slot, mp] = al_scr[mp] * acc_scr[slot, mp] + pv

    def step(j, last_scores_masked=False):
        accumulate(j - 1)
        softmax_update(j)
        scores(j + 1, masked=last_scores_masked)

    scores(0, masked=True)
    softmax_update(0)
    scores(1)

    def body(t, c):
        for u in range(unroll):
            step(1 + t * unroll + u)
        return c

    lax.fori_loop(0, (nq - 2) // unroll, body, 0)
    step(nq - 1, last_scores_masked=True)

    @pl.when(i >= 0)
    def _():
        accumulate(nq - 1)
        softmax_update(nq)

    @pl.when(i >= 0)
    def _():
        accumulate(nq)

    lp = lam_ref[...]
    lam = (jnp.exp(jnp.sum(lp[0:1] * lp[1:2], axis=1, keepdims=True))
           - jnp.exp(jnp.sum(lp[2:3] * lp[3:4], axis=1, keepdims=True)) + LAM_INIT)
    for slot in range(2):
        o = (acc_scr[slot, 0] * (1.0 / l_scr[slot, 0])
             - lam * (acc_scr[slot, 1] * (1.0 / l_scr[slot, 1])))
        ms = jnp.mean(o * o, axis=0, keepdims=True)
        y = o * lax.rsqrt(ms + EPS) * subln_ref[...] * (1.0 - LAM_INIT)
        o_ref[0, slot, 0] = y.T.astype(BF16)


def _attention(q, k, vt, lam, subln_col, B, S, tq, unroll):
    nq = S // tq
    nh = nq // 2
    assert nq % 2 == 0 and (nq - 2) % unroll == 0
    kern = functools.partial(_attn_kernel, tq=tq, nq=nq, unroll=unroll)
    return pl.pallas_call(
        kern,
        grid=(B, DA_HEADS, nh),
        in_specs=[
            pl.BlockSpec((tq, 128), lambda b, h, i: (b * nq + i, h)),
            pl.BlockSpec((tq, 128), lambda b, h, i: (b * nq + nq - 1 - i, h)),
            pl.BlockSpec((S, 128), lambda b, h, i: (b, h)),
            pl.BlockSpec((1, 128, S), lambda b, h, i: (b, h, 0)),
            _const_spec((4, DA_QK_DIM)),
            _const_spec((DA_V_DIM, 1)),
        ],
        out_specs=pl.BlockSpec((1, 2, 1, tq, 128), lambda b, h, i: (b, 0, i, 0, h)),
        out_shape=jax.ShapeDtypeStruct((B, 2, nh, tq, DA_WIDTH), BF16),
        scratch_shapes=[
            pltpu.VMEM((2, 2, tq, 128), BF16),
            pltpu.VMEM((2, tq, tq + 128), F32),
            pltpu.VMEM((2, 1, tq), F32),
            pltpu.VMEM((2, tq, tq + 128), BF16),
            pltpu.VMEM((2, 1, tq), F32),
            pltpu.VMEM((2, 2, 1, tq), F32),
            pltpu.VMEM((2, 2, 1, tq), F32),
            pltpu.VMEM((2, 2, DA_V_DIM, tq), F32),
        ],
        compiler_params=pltpu.CompilerParams(
            dimension_semantics=("arbitrary", "arbitrary", "arbitrary"),
            vmem_limit_bytes=VMEM_LIMIT),
        name="diffattn",
    )(q, q, k, vt, lam, subln_col)


def _split3(a):
    hi = a.astype(BF16)
    r = a - hi.astype(F32)
    mid = r.astype(BF16)
    lo = (r - mid.astype(F32)).astype(BF16)
    return hi, mid, lo


def _hgrn_kernel(hq_ref, hf_ref, hi_ref, og_ref, lbraw_ref, gn_ref, o_ref, st_scr, *, tc):
    @pl.when(pl.program_id(1) == 0)
    def _():
        st_scr[...] = jnp.zeros(st_scr.shape, F32)

    nchunk = tc // CHUNK
    lbr = lbraw_ref[...]
    e = jnp.exp(lbr - jnp.max(lbr, axis=0, keepdims=True))
    lb = e[0:1] / jnp.sum(e, axis=0, keepdims=True)

    hq = hq_ref[...].astype(F32)
    q = hq * jax.nn.sigmoid(hq)
    f = lb + (1.0 - lb) * jax.nn.sigmoid(hf_ref[...])
    logf = jnp.log(f)
    kk = 1.0 - f
    v = hi_ref[...]

    r = lax.broadcasted_iota(jnp.int32, (tc, tc), 0)
    c = lax.broadcasted_iota(jnp.int32, (tc, tc), 1)
    tril = (r >= c) & ((r // CHUNK) == (c // CHUNK))
    ltri = tril.astype(BF16)
    b = sum(jnp.dot(ltri, part, preferred_element_type=F32) for part in _split3(logf))

    qd = (q * jnp.exp(b)).astype(BF16)
    kd = (kk * jnp.exp(-b)).astype(BF16)

    og = og_ref[...].astype(F32)
    gate = og * jax.nn.sigmoid(og)
    gn = gn_ref[...]

    for h in range(HG_HEADS):
        hs = slice(h * HG_K, (h + 1) * HG_K)
        a = lax.dot_general(qd[:, hs], kd[:, hs], _NT, preferred_element_type=F32)
        a = jnp.where(tril, a, 0.0).astype(BF16)
        o_intra = jnp.dot(a, v[:, hs], preferred_element_type=F32)
        for n in range(nchunk):
            rs = slice(n * CHUNK, (n + 1) * CHUNK)
            b_c = b[rs, hs]
            b_last = b_c[CHUNK - 1:CHUNK, :]
            st = st_scr[h]
            o_inter = lax.dot_general(qd[rs, hs], st.astype(BF16), _NT,
                                      preferred_element_type=F32)
            o = o_intra[rs, :] + o_inter
            k_end = (kk[rs, hs] * jnp.exp(b_last - b_c)).astype(BF16)
            d_t = lax.dot_general(v[rs, hs], k_end, _TN, preferred_element_type=F32)
            st_scr[h] = st * jnp.exp(b_last) + d_t
            ms = jnp.mean(o * o, axis=-1, keepdims=True)
            y = o * lax.rsqrt(ms + EPS) * gn * gate[rs, hs]
            o_ref[rs, hs] = y.astype(BF16)


def _hgrn(hq, hf, hi, og, lbraw, gn, B, S, tc):
    T = B * S
    ns = S // tc
    row = lambda b, s: (b * ns + s, 0)
    kern = functools.partial(_hgrn_kernel, tc=tc)
    return pl.pallas_call(
        kern,
        grid=(B, ns),
        in_specs=[
            pl.BlockSpec((tc, 512), row),
            pl.BlockSpec((tc, 512), row),
            pl.BlockSpec((tc, 512), row),
            pl.BlockSpec((tc, 512), row),
            _const_spec((2, HG_WIDTH)),
            _const_spec((1, HG_V)),
        ],
        out_specs=pl.BlockSpec((tc, 512), row),
        out_shape=jax.ShapeDtypeStruct((T, HG_WIDTH), BF16),
        scratch_shapes=[pltpu.VMEM((HG_HEADS, HG_V, HG_K), F32)],
        compiler_params=pltpu.CompilerParams(
            dimension_semantics=("arbitrary", "arbitrary"),
            vmem_limit_bytes=VMEM_LIMIT),
        name="hgrn2",
    )(hq, hf, hi, og, lbraw, gn)


def _rms(x, g):
    ms = jnp.mean(x * x, axis=-1, keepdims=True)
    return x * lax.rsqrt(ms + EPS) * g


def _post_kernel(x_ref, oda_ref, ohg_ref, gates_ref, wda_ref, whg_ref, wout_ref,
                 nffn_ref, wfi_ref, wfo_ref, nfin_ref, out_ref, act_scr, *, fc):
    y_da = jnp.dot(oda_ref[...], wda_ref[...], preferred_element_type=F32)
    y_hg = jnp.dot(ohg_ref[...], whg_ref[...], preferred_element_type=F32)
    g_da = jax.nn.sigmoid(gates_ref[:, 0:1024].astype(F32))
    g_hg = jax.nn.sigmoid(gates_ref[:, 1024:2048].astype(F32))
    merged = (g_da * y_da + g_hg * y_hg).astype(BF16)
    h = x_ref[...] + jnp.dot(merged, wout_ref[...], preferred_element_type=F32)
    z = _rms(h, nffn_ref[...]).astype(BF16)
    for c0 in range(0, D_FF, fc):
        gate = jnp.dot(z, wfi_ref[:, c0:c0 + fc], preferred_element_type=F32)
        up = jnp.dot(z, wfi_ref[:, D_FF + c0:D_FF + c0 + fc], preferred_element_type=F32)
        act_scr[:, c0:c0 + fc] = (gate * jax.nn.sigmoid(gate) * up).astype(BF16)
    h = h + jnp.dot(act_scr[...], wfo_ref[...], preferred_element_type=F32)
    out_ref[...] = _rms(h, nfin_ref[...])


def _post(x2, oda, ohg, gates, wda, whg, wout, nffn, wfi, wfo, nfin, tm, fc):
    T = x2.shape[0]
    row = lambda i: (i, 0)
    B, _, nh, tq, _ = oda.shape
    sub = tq // tm
    oda = oda.reshape(B, 2, nh, sub, tm, DA_WIDTH)
    tiles_per_batch = 2 * nh * sub

    def oda_idx(i):
        b = i // tiles_per_batch
        r = i % tiles_per_batch
        qt = r // sub
        upper = qt >= nh
        return (b, jnp.where(upper, 1, 0), jnp.where(upper, 2 * nh - 1 - qt, qt), r % sub, 0, 0)

    kern = functools.partial(_post_kernel, fc=fc)
    return pl.pallas_call(
        kern,
        grid=(T // tm,),
        in_specs=[
            pl.BlockSpec((tm, D_MODEL), row),
            pl.BlockSpec((None, None, None, None, tm, DA_WIDTH), oda_idx),
            pl.BlockSpec((tm, HG_WIDTH), row),
            pl.BlockSpec((tm, 2048), row),
            _const_spec((DA_WIDTH, D_MODEL)),
            _const_spec((HG_WIDTH, D_MODEL)),
            _const_spec((D_MODEL, D_MODEL)),
            _const_spec((1, D_MODEL)),
            _const_spec((D_MODEL, 2 * D_FF)),
            _const_spec((D_FF, D_MODEL)),
            _const_spec((1, D_MODEL)),
        ],
        out_specs=pl.BlockSpec((tm, D_MODEL), row),
        out_shape=jax.ShapeDtypeStruct((T, D_MODEL), F32),
        scratch_shapes=[pltpu.VMEM((tm, D_FF), BF16)],
        compiler_params=pltpu.CompilerParams(
            dimension_semantics=("arbitrary",),
            vmem_limit_bytes=VMEM_LIMIT),
        name="post",
    )(x2, oda, ohg, gates, wda, whg, wout, nffn, wfi, wfo, nfin)


def kernel(x, norm_mix, w_in, da_lambda, da_subln, hg_lower_bound, hg_norm,
           w_da_out, w_hg_out, w_out, norm_ffn, w_ffn_in, w_ffn_out, norm_final):
    B, S, D = x.shape
    T = B * S
    x2 = x.reshape(T, D)
    w_in_b = w_in[0].astype(BF16)
    wvt = w_in[0][:, C_V:C_HQ].T.astype(BF16)

    q, k, vt, hq, hf, hi, og, gates = _inproj(
        x2, norm_mix[0].reshape(1, D), w_in_b, wvt, B, S, tm=512)

    oda = _attention(q, k, vt, da_lambda[0], da_subln[0].reshape(DA_V_DIM, 1), B, S, tq=512, unroll=2)
    ohg = _hgrn(hq, hf, hi, og, hg_lower_bound, hg_norm[0].reshape(1, HG_V), B, S, tc=256)

    out = _post(x2, oda, ohg, gates,
                w_da_out[0].astype(BF16), w_hg_out[0].astype(BF16), w_out[0].astype(BF16),
                norm_ffn[0].reshape(1, D), w_ffn_in[0].astype(BF16), w_ffn_out[0].astype(BF16),
                norm_final.reshape(1, D), tm=256, fc=256)
    return out.reshape(B, S, D)
```

```python
import functools
import math

import jax
import jax.numpy as jnp
from jax import lax
from jax.experimental import pallas as pl
from jax.experimental.pallas import tpu as pltpu

F32 = jnp.float32
BF16 = jnp.bfloat16

D_MODEL = 1024
CHUNK = 64
EPS = 1e-6
DA_HEADS = 4
DA_QK_DIM = 64
DA_V_DIM = 128
DA_WIDTH = DA_HEADS * DA_V_DIM
HG_HEADS = 4
HG_K = 128
HG_V = 128
HG_WIDTH = HG_HEADS * HG_V
D_FF = 2816
LAM_INIT = 0.8 - 0.6 * math.exp(-0.3 * 0)

C_Q, C_K, C_V, C_HQ, C_HF, C_HI, C_OG, C_GDA, C_GHG, C_END = (
    0, 512, 1024, 1536, 2048, 2560, 3072, 3584, 4608, 5632)

VMEM_LIMIT = 56 * 1024 * 1024
NEG = float(jnp.finfo(jnp.float32).min)

_NT = (((1,), (1,)), ((), ()))
_TN = (((0,), (0,)), ((), ()))


def _const_spec(shape):
    nd = len(shape)
    return pl.BlockSpec(shape, lambda *_: (0,) * nd, pipeline_mode=pl.Buffered(1))


def _split3(a):
    hi = a.astype(BF16)
    r = a - hi.astype(F32)
    mid = r.astype(BF16)
    lo = (r - mid.astype(F32)).astype(BF16)
    return hi, mid, lo


def _silu(x):
    return x * jax.nn.sigmoid(x)


def _hgrn_block(hq, hf, hi, og, lb, gn, st_scr, o_ref, row0):
    n = hq.shape[0]
    q = _silu(hq)
    f = lb + (1.0 - lb) * jax.nn.sigmoid(hf)
    logf = jnp.log(f)
    kk = 1.0 - f
    v = hi.astype(BF16)

    r = lax.broadcasted_iota(jnp.int32, (n, n), 0)
    c = lax.broadcasted_iota(jnp.int32, (n, n), 1)
    tril = (r >= c) & ((r // CHUNK) == (c // CHUNK))
    ltri = tril.astype(BF16)
    b = sum(jnp.dot(ltri, part, preferred_element_type=F32) for part in _split3(logf))

    qd = (q * jnp.exp(b)).astype(BF16)
    kd = (kk * jnp.exp(-b)).astype(BF16)
    gate = _silu(og)

    for h in range(HG_HEADS):
        hs = slice(h * HG_K, (h + 1) * HG_K)
        a = lax.dot_general(qd[:, hs], kd[:, hs], _NT, preferred_element_type=F32)
        a = jnp.where(tril, a, 0.0).astype(BF16)
        o_intra = jnp.dot(a, v[:, hs], preferred_element_type=F32)
        for ck in range(n // CHUNK):
            rs = slice(ck * CHUNK, (ck + 1) * CHUNK)
            b_c = b[rs, hs]
            b_last = b_c[CHUNK - 1:CHUNK, :]
            st = st_scr[h]
            o_inter = lax.dot_general(qd[rs, hs], st.astype(BF16), _NT,
                                      preferred_element_type=F32)
            o = o_intra[rs, :] + o_inter
            k_end = (kk[rs, hs] * jnp.exp(b_last - b_c)).astype(BF16)
            d_t = lax.dot_general(v[rs, hs], k_end, _TN, preferred_element_type=F32)
            st_scr[h] = st * jnp.exp(b_last) + d_t
            ms = jnp.mean(o * o, axis=-1, keepdims=True)
            y = o * lax.rsqrt(ms + EPS) * gn * gate[rs, hs]
            o_ref[row0 + ck * CHUNK:row0 + (ck + 1) * CHUNK, hs] = y.astype(BF16)


def _inproj_kernel(x_ref, g_ref, w_ref, lbraw_ref, gn_ref,
                   q_ref, k_ref, vt_ref, gates_ref, ohg_ref, st_scr, *, hg_block):
    @pl.when(pl.program_id(1) == 0)
    def _():
        st_scr[...] = jnp.zeros(st_scr.shape, F32)

    x = x_ref[...]
    ms = jnp.mean(x * x, axis=-1, keepdims=True)
    u = (x * lax.rsqrt(ms + EPS) * g_ref[...]).astype(BF16)

    def seg(c0, c1):
        return jnp.dot(u, w_ref[:, c0:c1], preferred_element_type=F32)

    q_ref[...] = (seg(C_Q, C_K) * (DA_QK_DIM ** -0.5 * math.log2(math.e))).astype(BF16)
    k_ref[...] = seg(C_K, C_V).astype(BF16)
    vt_ref[0] = seg(C_V, C_HQ).T.astype(BF16)
    gates_ref[:, 0:D_MODEL] = seg(C_GDA, C_GHG).astype(BF16)
    gates_ref[:, D_MODEL:2 * D_MODEL] = seg(C_GHG, C_END).astype(BF16)

    lbr = lbraw_ref[...]
    e = jnp.exp(lbr - jnp.max(lbr, axis=0, keepdims=True))
    lb = e[0:1] / jnp.sum(e, axis=0, keepdims=True)
    gn = gn_ref[...]
    hq, hf, hi, og = seg(C_HQ, C_HF), seg(C_HF, C_HI), seg(C_HI, C_OG), seg(C_OG, C_GDA)
    for blk in range(x.shape[0] // hg_block):
        rows = slice(blk * hg_block, (blk + 1) * hg_block)
        _hgrn_block(hq[rows], hf[rows], hi[rows], og[rows], lb, gn, st_scr, ohg_ref,
                    blk * hg_block)


def _inproj(x2, g, w, lbraw, gn, B, S, tm, hg_block):
    T = B * S
    ns = S // tm
    row = lambda b, s: (b * ns + s, 0)
    tok = lambda n, dt: jax.ShapeDtypeStruct((T, n), dt)
    kern = functools.partial(_inproj_kernel, hg_block=hg_block)
    return pl.pallas_call(
        kern,
        grid=(B, ns),
        in_specs=[
            pl.BlockSpec((tm, D_MODEL), row),
            _const_spec((1, D_MODEL)),
            _const_spec((D_MODEL, C_END)),
            _const_spec(lbraw.shape),
            _const_spec((1, HG_V)),
        ],
        out_specs=[
            pl.BlockSpec((tm, C_K - C_Q), row),
            pl.BlockSpec((tm, C_V - C_K), row),
            pl.BlockSpec((1, DA_WIDTH, tm), lambda b, s: (b, 0, s)),
            pl.BlockSpec((tm, 2 * D_MODEL), row),
            pl.BlockSpec((tm, HG_WIDTH), row),
        ],
        out_shape=[
            tok(C_K - C_Q, BF16), tok(C_V - C_K, BF16),
            jax.ShapeDtypeStruct((B, DA_WIDTH, S), BF16),
            tok(2 * D_MODEL, BF16), tok(HG_WIDTH, BF16),
        ],
        scratch_shapes=[pltpu.VMEM((HG_HEADS, HG_V, HG_K), F32)],
        compiler_params=pltpu.CompilerParams(
            dimension_semantics=("arbitrary", "arbitrary"),
            vmem_limit_bytes=VMEM_LIMIT),
        name="inproj_hgrn2",
    )(x2, g, w, lbraw, gn)


def _attn_kernel(qa_ref, qb_ref, k_ref, vt_ref, lam_ref, subln_ref, o_ref,
                 q_scr, s_scr, mt_scr, p_scr, al_scr, m_scr, l_scr, acc_scr, *, tq, nq, unroll):
    i = pl.program_id(2)
    for slot, ref in enumerate((qa_ref, qb_ref)):
        q = ref[...]
        lane = lax.broadcasted_iota(jnp.int32, q.shape, 1)
        zero = jnp.zeros_like(q)
        q_scr[slot, 0] = jnp.where(lane < DA_QK_DIM, q, zero)
        q_scr[slot, 1] = jnp.where(lane >= DA_QK_DIM, q, zero)

    m_scr[...] = jnp.full(m_scr.shape, NEG, F32)
    l_scr[...] = jnp.zeros(l_scr.shape, F32)
    acc_scr[...] = jnp.zeros(acc_scr.shape, F32)

    def tile_of(j):
        if isinstance(j, int) and j == 0:
            return jnp.minimum(i, 0), i
        if isinstance(j, int) and j == nq:
            return 1 - jnp.minimum(i, 0), nq - 1 - i
        in_a = j <= i
        return (jnp.where(in_a, 0, 1),
                jnp.where(in_a, jnp.where(j == 0, i, j - 1), j - 1 - i))

    def scores(j, masked=False):
        slot, kt = tile_of(j)
        start = pl.multiple_of(kt * tq, tq)
        k = k_ref[pl.ds(start, tq), :]
        if masked:
            qchunk = lax.broadcasted_iota(jnp.int32, (1, tq), 1) // CHUNK
        for mp in range(2):
            s = lax.dot_general(k, q_scr[slot, mp], _NT, preferred_element_type=F32)
            if masked:
                s = jnp.concatenate(
                    [jnp.where(qchunk >= c, s[c * CHUNK:(c + 1) * CHUNK, :], NEG)
                     for c in range(tq // CHUNK)], axis=0)
            s_scr[mp] = s
            mt_scr[mp] = jnp.max(s, axis=0, keepdims=True)

    def softmax_update(j):
        slot, _ = tile_of(j)
        for mp in range(2):
            m_old = m_scr[slot, mp]
            m_new = jnp.maximum(m_old, mt_scr[mp])
            alpha = jnp.exp2(m_old - m_new)
            p = jnp.exp2(s_scr[mp] - m_new)
            l_scr[slot, mp] = alpha * l_scr[slot, mp] + jnp.sum(p, axis=0, keepdims=True)
            p_scr[mp] = p.astype(BF16)
            al_scr[mp] = alpha
            m_scr[slot, mp] = m_new

    def accumulate(j):
        slot, kt = tile_of(j)
        start = pl.multiple_of(kt * tq, tq)
        vt = vt_ref[0, :, pl.ds(start, tq)]
        for mp in range(2):
            pv = jnp.dot(vt, p_scr[mp], preferred_element_type=F32)
            acc_scr[slot, mp] = al_scr[mp] * acc_scr[slot, mp] + pv

    def step(j, last_scores_masked=False):
        accumulate(j - 1)
        softmax_update(j)
        scores(j + 1, masked=last_scores_masked)

    scores(0, masked=True)
    softmax_update(0)
    scores(1)

    def body(t, c):
        for u in range(unroll):
            step(1 + t * unroll + u)
        return c

    lax.fori_loop(0, (nq - 2) // unroll, body, 0)
    step(nq - 1, last_scores_masked=True)

    @pl.when(i >= 0)
    def _():
        accumulate(nq - 1)
        softmax_update(nq)

    @pl.when(i >= 0)
    def _():
        accumulate(nq)

    lp = lam_ref[...]
    lam = (jnp.exp(jnp.sum(lp[0:1] * lp[1:2], axis=1, keepdims=True))
           - jnp.exp(jnp.sum(lp[2:3] * lp[3:4], axis=1, keepdims=True)) + LAM_INIT)
    for slot in range(2):
        o = (acc_scr[slot, 0] * (1.0 / l_scr[slot, 0])
             - lam * (acc_scr[slot, 1] * (1.0 / l_scr[slot, 1])))
        ms = jnp.mean(o * o, axis=0, keepdims=True)
        y = o * lax.rsqrt(ms + EPS) * subln_ref[...] * (1.0 - LAM_INIT)
        o_ref[0, slot, 0] = y.T.astype(BF16)


def _attention(q, k, vt, lam, subln_col, B, S, tq, unroll):
    nq = S // tq
    nh = nq // 2
    assert nq % 2 == 0 and (nq - 2) % unroll == 0
    kern = functools.partial(_attn_kernel, tq=tq, nq=nq, unroll=unroll)
    return pl.pallas_call(
        kern,
        grid=(B, DA_HEADS, nh),
        in_specs=[
            pl.BlockSpec((tq, 128), lambda b, h, i: (b * nq + i, h)),
            pl.BlockSpec((tq, 128), lambda b, h, i: (b * nq + nq - 1 - i, h)),
            pl.BlockSpec((S, 128), lambda b, h, i: (b, h)),
            pl.BlockSpec((1, 128, S), lambda b, h, i: (b, h, 0)),
            _const_spec((4, DA_QK_DIM)),
            _const_spec((DA_V_DIM, 1)),
        ],
        out_specs=pl.BlockSpec((1, 2, 1, tq, 128), lambda b, h, i: (b, 0, i, 0, h)),
        out_shape=jax.ShapeDtypeStruct((B, 2, nh, tq, DA_WIDTH), BF16),
        scratch_shapes=[
            pltpu.VMEM((2, 2, tq, 128), BF16),
            pltpu.VMEM((2, tq, tq), F32),
            pltpu.VMEM((2, 1, tq), F32),
            pltpu.VMEM((2, tq, tq), BF16),
            pltpu.VMEM((2, 1, tq), F32),
            pltpu.VMEM((2, 2, 1, tq), F32),
            pltpu.VMEM((2, 2, 1, tq), F32),
            pltpu.VMEM((2, 2, DA_V_DIM, tq), F32),
        ],
        compiler_params=pltpu.CompilerParams(
            dimension_semantics=("arbitrary", "arbitrary", "arbitrary"),
            vmem_limit_bytes=VMEM_LIMIT),
        name="diffattn",
    )(q, q, k, vt, lam, subln_col)


def _rms(x, g):
    ms = jnp.mean(x * x, axis=-1, keepdims=True)
    return x * lax.rsqrt(ms + EPS) * g


def _post_kernel(x_ref, oda_ref, ohg_ref, gates_ref, wda_ref, whg_ref, wout_ref,
                 nffn_ref, wfi_ref, wfo_ref, nfin_ref, out_ref, act_scr, *, fc):
    y_da = jnp.dot(oda_ref[...], wda_ref[...], preferred_element_type=F32)
    y_hg = jnp.dot(ohg_ref[...], whg_ref[...], preferred_element_type=F32)
    g_da = jax.nn.sigmoid(gates_ref[:, 0:D_MODEL].astype(F32))
    g_hg = jax.nn.sigmoid(gates_ref[:, D_MODEL:2 * D_MODEL].astype(F32))
    merged = (g_da * y_da + g_hg * y_hg).astype(BF16)
    h = x_ref[...] + jnp.dot(merged, wout_ref[...], preferred_element_type=F32)
    z = _rms(h, nffn_ref[...]).astype(BF16)
    for c0 in range(0, D_FF, fc):
        gate = jnp.dot(z, wfi_ref[:, c0:c0 + fc], preferred_element_type=F32)
        up = jnp.dot(z, wfi_ref[:, D_FF + c0:D_FF + c0 + fc], preferred_element_type=F32)
        act_scr[:, c0:c0 + fc] = (gate * jax.nn.sigmoid(gate) * up).astype(BF16)
    h = h + jnp.dot(act_scr[...], wfo_ref[...], preferred_element_type=F32)
    out_ref[...] = _rms(h, nfin_ref[...])


def _post(x2, oda, ohg, gates, wda, whg, wout, nffn, wfi, wfo, nfin, tm, fc):
    T = x2.shape[0]
    row = lambda i: (i, 0)
    B, _, nh, tq, _ = oda.shape
    sub = tq // tm
    oda = oda.reshape(B, 2, nh, sub, tm, DA_WIDTH)
    tiles_per_batch = 2 * nh * sub

    def oda_idx(i):
        b = i // tiles_per_batch
        r = i % tiles_per_batch
        qt = r // sub
        upper = qt >= nh
        return (b, jnp.where(upper, 1, 0), jnp.where(upper, 2 * nh - 1 - qt, qt), r % sub, 0, 0)

    kern = functools.partial(_post_kernel, fc=fc)
    return pl.pallas_call(
        kern,
        grid=(T // tm,),
        in_specs=[
            pl.BlockSpec((tm, D_MODEL), row),
            pl.BlockSpec((None, None, None, None, tm, DA_WIDTH), oda_idx),
            pl.BlockSpec((tm, HG_WIDTH), row),
            pl.BlockSpec((tm, 2 * D_MODEL), row),
            _const_spec((DA_WIDTH, D_MODEL)),
            _const_spec((HG_WIDTH, D_MODEL)),
            _const_spec((D_MODEL, D_MODEL)),
            _const_spec((1, D_MODEL)),
            _const_spec((D_MODEL, 2 * D_FF)),
            _const_spec((D_FF, D_MODEL)),
            _const_spec((1, D_MODEL)),
        ],
        out_specs=pl.BlockSpec((tm, D_MODEL), row),
        out_shape=jax.ShapeDtypeStruct((T, D_MODEL), F32),
        scratch_shapes=[pltpu.VMEM((tm, D_FF), BF16)],
        compiler_params=pltpu.CompilerParams(
            dimension_semantics=("arbitrary",),
            vmem_limit_bytes=VMEM_LIMIT),
        name="post",
    )(x2, oda, ohg, gates, wda, whg, wout, nffn, wfi, wfo, nfin)


def kernel(x, norm_mix, w_in, da_lambda, da_subln, hg_lower_bound, hg_norm,
           w_da_out, w_hg_out, w_out, norm_ffn, w_ffn_in, w_ffn_out, norm_final):
    B, S, D = x.shape
    T = B * S
    x2 = x.reshape(T, D)
    w_in_b = w_in[0].astype(BF16)

    q, k, vt, gates, ohg = _inproj(
        x2, norm_mix[0].reshape(1, D), w_in_b, hg_lower_bound, hg_norm[0].reshape(1, HG_V),
        B, S, tm=512, hg_block=256)

    oda = _attention(q, k, vt, da_lambda[0], da_subln[0].reshape(DA_V_DIM, 1), B, S, tq=512, unroll=2)

    out = _post(x2, oda, ohg, gates,
                w_da_out[0].astype(BF16), w_hg_out[0].astype(BF16), w_out[0].astype(BF16),
                norm_ffn[0].reshape(1, D), w_ffn_in[0].astype(BF16), w_ffn_out[0].astype(BF16),
                norm_final.reshape(1, D), tm=512, fc=256)
    return out.reshape(B, S, D)
```
